```python
import math
import jax, jax.numpy as jnp
from jax import lax
import numpy as np

D_MODEL = 1024
BATCH = 8
SEQ = 4096
DEPTH = 4
DEC_BATCH = 8
DEC_SEQ = 2048
PAST_LEN = 128

N_HEADS_ATT = 8
HEAD_DIM = 64
V_HEAD_DIM = 2 * HEAD_DIM
D_QK = N_HEADS_ATT * 2 * HEAD_DIM
D_ATT = N_HEADS_ATT * V_HEAD_DIM
ROT_DIM = HEAD_DIM // 4
ROPE_THETA = 500000.0
Q_BLOCK = 128
D_RNN = D_MODEL
N_RNN_BLOCKS = 16
RNN_BLOCK = D_RNN // N_RNN_BLOCKS
CONV_W = 4
CONV_LEFT = 2
RG_C = 8.0
N_BRANCH = 2
D_FF = 4 * D_MODEL
EPS = 1e-6
D_IN = 2 * D_QK + D_ATT + 2 * D_RNN + N_BRANCH * D_MODEL
SPLITS = (D_QK, 2 * D_QK, 2 * D_QK + D_ATT, 2 * D_QK + D_ATT + D_RNN, 2 * D_QK + D_ATT + 2 * D_RNN)

kernel_name = "hybrid_diffattn_rglru_encoder"


def rmsnorm(x, g):
    xf = x.astype(jnp.float32)
    y = xf * lax.rsqrt(jnp.mean(xf * xf, axis=-1, keepdims=True) + EPS) * g.astype(jnp.float32)
    return y.astype(x.dtype)


def rope_partial(x, cos, sin):
    half = ROT_DIM // 2
    x1 = x[..., :half].astype(jnp.float32)
    x2 = x[..., half:ROT_DIM].astype(jnp.float32)
    rot = jnp.concatenate([x1 * cos - x2 * sin, x2 * cos + x1 * sin], axis=-1).astype(x.dtype)
    return jnp.concatenate([rot, x[..., ROT_DIM:]], axis=-1)


def diff_attention(q1, q2, k1, k2, v, lam):
    b, s, h, dh = q1.shape
    nb = s // Q_BLOCK
    scale = dh ** -0.5
    qb = jnp.stack([q1, q2], 0).reshape(2, b, nb, Q_BLOCK, h, dh).transpose(2, 0, 1, 3, 4, 5)
    k = jnp.stack([k1, k2], 0)

    def block(qblk):
        sc = jnp.einsum('cbqhd,cbkhd->cbhqk', qblk, k).astype(jnp.float32) * scale
        p = jax.nn.softmax(sc, axis=-1)
        w = p[0] - lam * p[1]
        return jnp.einsum('bhqk,bkhe->bqhe', w.astype(v.dtype), v)

    out = lax.map(block, qb)
    return out.transpose(1, 0, 2, 3, 4).reshape(b, s, h, V_HEAD_DIM)


def centred_depthwise_conv(x, w, bias):
    s = x.shape[1]
    xp = jnp.pad(x, ((0, 0), (CONV_LEFT, CONV_W - 1 - CONV_LEFT), (0, 0)))
    out = bias
    for j in range(CONV_W):
        out = out + xp[:, j:j + s, :] * w[j]
    return out


def rg_lru(x, w_gate, b_gate, lam_param, reverse):
    b, s, _ = x.shape
    xb = x.reshape(b, s, N_RNN_BLOCKS, RNN_BLOCK)
    gates = jnp.einsum('bsnc,gncd->gbsnd', xb, w_gate).reshape(2, b, s, D_RNN)
    gates = gates.astype(jnp.float32) + b_gate.astype(jnp.float32)[:, None, None, :]
    r = jax.nn.sigmoid(gates[0])
    i = jax.nn.sigmoid(gates[1])
    log_a = -RG_C * r * jax.nn.softplus(-lam_param.astype(jnp.float32))
    a = jnp.exp(log_a)
    u = jnp.sqrt(-jnp.expm1(2.0 * log_a)) * i * x.astype(jnp.float32)

    def combine(left, right):
        a_l, h_l = left
        a_r, h_r = right
        return a_l * a_r, a_r * h_l + h_r

    _, h = lax.associative_scan(combine, (a, u), axis=1, reverse=reverse)
    return h.astype(x.dtype)


def encoder_layer(x, l, norm1_g, w_in, b_gate, q_norm_g, k_norm_g, lam_vecs, subln_g,
                  conv_w, conv_b, rg_w, rg_b, rg_L, w_branch_att, w_branch_rec, w_out,
                  norm2_g, w_ff1, w_ff2):
    b, s, d = x.shape
    lam_init = 0.8 - 0.6 * math.exp(-0.3 * l)
    n = rmsnorm(x, norm1_g)
    proj = n @ w_in
    q, k, v, xr, yr, g = jnp.split(proj, SPLITS, axis=-1)

    q = rmsnorm(q.reshape(b, s, N_HEADS_ATT, 2, HEAD_DIM), q_norm_g)
    k = rmsnorm(k.reshape(b, s, N_HEADS_ATT, 2, HEAD_DIM), k_norm_g)
    pos = jnp.arange(s, dtype=jnp.float32)
    inv_freq = ROPE_THETA ** (-jnp.arange(0, ROT_DIM, 2, dtype=jnp.float32) / ROT_DIM)
    ang = pos[:, None] * inv_freq[None, :]
    cos = jnp.cos(ang)[:, None, None, :]
    sin = jnp.sin(ang)[:, None, None, :]
    q = rope_partial(q, cos, sin)
    k = rope_partial(k, cos, sin)
    v = v.reshape(b, s, N_HEADS_ATT, V_HEAD_DIM)
    lv = lam_vecs.astype(jnp.float32)
    lam = jnp.exp(jnp.sum(lv[0] * lv[1])) - jnp.exp(jnp.sum(lv[2] * lv[3])) + lam_init
    att = diff_attention(q[..., 0, :], q[..., 1, :], k[..., 0, :], k[..., 1, :], v, lam)
    att = (rmsnorm(att, subln_g) * (1.0 - lam_init)).astype(x.dtype).reshape(b, s, D_ATT)

    xc = centred_depthwise_conv(xr, conv_w, conv_b)
    hr = rg_lru(xc, rg_w[0], rg_b[0], rg_L[0], False) + rg_lru(xc, rg_w[1], rg_b[1], rg_L[1], True)
    rec = hr * jax.nn.gelu(yr)

    gates = jax.nn.sigmoid(g.astype(jnp.float32) + b_gate.astype(jnp.float32)).reshape(b, s, N_BRANCH, d)
    merged = (gates[..., 0, :] * (att @ w_branch_att) + gates[..., 1, :] * (rec @ w_branch_rec)).astype(x.dtype)
    x = x + merged @ w_out

    h2 = rmsnorm(x, norm2_g)
    return x + jnp.square(jax.nn.relu(h2 @ w_ff1)) @ w_ff2


def trunk(x, norm1_g, w_in, b_gate, q_norm_g, k_norm_g, lam_vecs, subln_g, conv_w, conv_b,
          rg_w, rg_b, rg_L, w_branch_att, w_branch_rec, w_out, norm2_g, w_ff1, w_ff2):
    for l in range(DEPTH):
        x = encoder_layer(x, l, norm1_g[l], w_in[l], b_gate[l], q_norm_g[l], k_norm_g[l], lam_vecs[l],
                          subln_g[l], conv_w[l], conv_b[l], rg_w[l], rg_b[l], rg_L[l], w_branch_att[l],
                          w_branch_rec[l], w_out[l], norm2_g[l], w_ff1[l], w_ff2[l])
    return x


def setup_inputs(seed: int = 0) -> dict:
    key = jax.random.key(seed)
    ks = jax.random.split(key, 22)
    f32 = jnp.float32
    nrm = lambda k, shape, scale: jax.random.normal(k, shape, f32) * scale
    u = jax.random.uniform(ks[13], (DEPTH, 2, D_RNN), f32, 0.9, 0.999)
    a_base = u ** (1.0 / RG_C)
    rg_L = jnp.log(a_base) - jnp.log1p(-a_base)
    return {
        "x_prompt": nrm(ks[0], (BATCH, SEQ, D_MODEL), 1.0),
        "x_sample": nrm(ks[1], (DEC_BATCH, DEC_SEQ, D_MODEL), 1.0),
        "norm1_g": 1.0 + nrm(ks[2], (DEPTH, D_MODEL), 0.02),
        "w_in": nrm(ks[3], (DEPTH, D_MODEL, D_IN), D_MODEL ** -0.5),
        "b_gate": nrm(ks[4], (DEPTH, N_BRANCH * D_MODEL), 0.02),
        "q_norm_g": 1.0 + nrm(ks[5], (DEPTH, HEAD_DIM), 0.02),
        "k_norm_g": 1.0 + nrm(ks[6], (DEPTH, HEAD_DIM), 0.02),
        "lam_vecs": nrm(ks[7], (DEPTH, 4, HEAD_DIM), 0.1),
        "subln_g": 1.0 + nrm(ks[8], (DEPTH, V_HEAD_DIM), 0.02),
        "conv_w": nrm(ks[9], (DEPTH, CONV_W, D_RNN), CONV_W ** -0.5),
        "conv_b": nrm(ks[10], (DEPTH, D_RNN), 0.02),
        "rg_w": nrm(ks[11], (DEPTH, 2, 2, N_RNN_BLOCKS, RNN_BLOCK, RNN_BLOCK), RNN_BLOCK ** -0.5),
        "rg_b": nrm(ks[12], (DEPTH, 2, 2, D_RNN), 0.02),
        "rg_L": rg_L,
        "w_branch_att": nrm(ks[14], (DEPTH, D_ATT, D_MODEL), D_ATT ** -0.5),
        "w_branch_rec": nrm(ks[15], (DEPTH, D_RNN, D_MODEL), D_RNN ** -0.5),
        "w_out": nrm(ks[16], (DEPTH, D_MODEL, D_MODEL), D_MODEL ** -0.5),
        "norm2_g": 1.0 + nrm(ks[17], (DEPTH, D_MODEL), 0.02),
        "w_ff1": nrm(ks[18], (DEPTH, D_MODEL, D_FF), D_MODEL ** -0.5),
        "w_ff2": nrm(ks[19], (DEPTH, D_FF, D_MODEL), D_FF ** -0.5),
    }


def reference(x_prompt, x_sample, norm1_g, w_in, b_gate, q_norm_g, k_norm_g, lam_vecs, subln_g,
              conv_w, conv_b, rg_w, rg_b, rg_L, w_branch_att, w_branch_rec, w_out, norm2_g,
              w_ff1, w_ff2):
    y_prompt = trunk(x_prompt, norm1_g, w_in, b_gate, q_norm_g, k_norm_g, lam_vecs, subln_g, conv_w,
                     conv_b, rg_w, rg_b, rg_L, w_branch_att, w_branch_rec, w_out, norm2_g, w_ff1, w_ff2)
    y_sample = trunk(x_sample, norm1_g, w_in, b_gate, q_norm_g, k_norm_g, lam_vecs, subln_g, conv_w,
                     conv_b, rg_w, rg_b, rg_L, w_branch_att, w_branch_rec, w_out, norm2_g, w_ff1, w_ff2)
    return (y_prompt, y_sample)
```

```python
import functools
import math

import jax
import jax.numpy as jnp
from jax import lax
from jax.experimental import pallas as pl
from jax.experimental.pallas import tpu as pltpu

F32 = jnp.float32
BF16 = jnp.bfloat16

D_MODEL = 1024
DEPTH = 4
N_HEADS = 8
HEAD_DIM = 64
V_HEAD_DIM = 2 * HEAD_DIM
ROT_DIM = HEAD_DIM // 4
ROPE_THETA = 500000.0
N_RNN_BLOCKS = 16
RNN_BLOCK = D_MODEL // N_RNN_BLOCKS
RG_C = 8.0
D_FF = 4 * D_MODEL
EPS = 1e-6
D_IN = 7 * D_MODEL
BATCH = 8

LANES = 128
GATE_GROUP = 256
N_GATE_GROUPS = D_MODEL // GATE_GROUP

ROW_TILE = 512
Q_TILE = 256
SCAN_STEPS = 64
VMEM_LIMIT = 56 * 1024 * 1024


def _resident(shape):
    nd = len(shape)
    return pl.BlockSpec(shape, lambda *_: (0,) * nd, pipeline_mode=pl.Buffered(1))


def _params(semantics):
    return pltpu.CompilerParams(dimension_semantics=semantics, vmem_limit_bytes=VMEM_LIMIT)


def _in_proj_kernel(x_ref, g1_ref, w_ref, qkg_ref, ones_ref, cos_ref, sin_ref, qkv_ref, rest_ref):
    x = x_ref[...]
    ms = jnp.mean(x * x, axis=-1, keepdims=True)
    n = (x * lax.rsqrt(ms + EPS) * g1_ref[...]).astype(BF16)
    lane = lax.broadcasted_iota(jnp.int32, (1, LANES), 1)
    first_half = (lane % HEAD_DIM) < (ROT_DIM // 2)
    cos_t = cos_ref[...]
    sin_t = sin_ref[...]
    for j in range(2):
        acc = jnp.dot(n, w_ref[:, j * D_MODEL:(j + 1) * D_MODEL], preferred_element_type=F32)
        sq = (acc * acc).astype(BF16)
        gain = qkg_ref[j]
        for c in range(N_GATE_GROUPS):
            c0 = c * GATE_GROUP
            ss = jnp.dot(sq[:, c0:c0 + GATE_GROUP], ones_ref[...], preferred_element_type=F32)
            inv = lax.rsqrt(ss * (1.0 / HEAD_DIM) + EPS)
            for h in range(GATE_GROUP // LANES):
                lo = c0 + h * LANES
                y = acc[:, lo:lo + LANES] * inv[:, h * LANES:(h + 1) * LANES] * gain
                rot = jnp.where(first_half,
                                pltpu.roll(y, LANES - ROT_DIM // 2, 1),
                                pltpu.roll(y, ROT_DIM // 2, 1))
                out = y * cos_t + rot * sin_t
                qkv_ref[:, j * D_MODEL + lo:j * D_MODEL + lo + LANES] = out.astype(BF16)
    acc = jnp.dot(n, w_ref[:, 2 * D_MODEL:3 * D_MODEL], preferred_element_type=F32)
    qkv_ref[:, 2 * D_MODEL:3 * D_MODEL] = acc.astype(BF16)
    for j in range(3, 7):
        rest_ref[:, (j - 3) * D_MODEL:(j - 2) * D_MODEL] = jnp.dot(
            n, w_ref[:, j * D_MODEL:(j + 1) * D_MODEL], preferred_element_type=F32)


def _in_proj(x, g1, w_in, qk_gain, ones_bd, cos_t, sin_t):
    rows = x.shape[0]
    tm = min(ROW_TILE, rows)
    grid = (rows // tm,)
    row_spec = lambda width: pl.BlockSpec((tm, width), lambda i: (i, 0))
    return pl.pallas_call(
        _in_proj_kernel,
        grid=grid,
        in_specs=[row_spec(D_MODEL), _resident((1, D_MODEL)), _resident((D_MODEL, D_IN)),
                  _resident((2, 1, LANES)), _resident((GATE_GROUP, GATE_GROUP)),
                  row_spec(LANES), row_spec(LANES)],
        out_specs=[row_spec(3 * D_MODEL), row_spec(4 * D_MODEL)],
        out_shape=[jax.ShapeDtypeStruct((rows, 3 * D_MODEL), BF16),
                   jax.ShapeDtypeStruct((rows, 4 * D_MODEL), F32)],
        compiler_params=_params(("arbitrary",)),
        name="in_proj",
    )(x, g1, w_in, qk_gain, ones_bd, cos_t, sin_t)


def _attn_kernel(lv_ref, q_ref, k_ref, v_ref, sg_ref, o_ref, *, lam_init):
    lv = lv_ref[...]
    lam = (jnp.exp(jnp.sum(lv[0:1] * lv[1:2], axis=-1, keepdims=True))
           - jnp.exp(jnp.sum(lv[2:3] * lv[3:4], axis=-1, keepdims=True)) + lam_init)
    q = q_ref[...]
    lane = lax.broadcasted_iota(jnp.int32, q.shape, 1)
    zero = jnp.zeros_like(q)
    k = k_ref[...]
    v = v_ref[...]
    nt = (((1,), (1,)), ((), ()))
    probs = []
    for c in range(2):
        qc = jnp.where((lane >= c * HEAD_DIM) & (lane < (c + 1) * HEAD_DIM), q, zero)
        s = lax.dot_general(qc, k, nt, preferred_element_type=F32)
        p = jnp.exp(s - jnp.max(s, axis=-1, keepdims=True))
        probs.append((p, jnp.sum(p, axis=-1, keepdims=True)))
    (p0, l0), (p1, l1) = probs
    w = p0 * (1.0 / l0) - p1 * (lam / l1)
    o = jnp.dot(w.astype(BF16), v, preferred_element_type=F32)
    ms = jnp.mean(o * o, axis=-1, keepdims=True)
    o_ref[...] = (o * lax.rsqrt(ms + EPS) * sg_ref[...] * (1.0 - lam_init)).astype(o_ref.dtype)


def _attention(qkv, lam_vecs, subln_g, lam_init, seq):
    view = qkv.reshape(seq, BATCH * 3 * D_MODEL)
    tq = min(Q_TILE, seq)
    blocks_per_seq = 3 * N_HEADS
    grid = (BATCH * N_HEADS, seq // tq)

    def col(bh, part):
        return (bh // N_HEADS) * blocks_per_seq + part * N_HEADS + bh % N_HEADS

    out = pl.pallas_call(
        functools.partial(_attn_kernel, lam_init=lam_init),
        grid=grid,
        in_specs=[_resident((4, HEAD_DIM)),
                  pl.BlockSpec((tq, LANES), lambda bh, qi: (qi, col(bh, 0))),
                  pl.BlockSpec((seq, LANES), lambda bh, qi: (0, col(bh, 1))),
                  pl.BlockSpec((seq, LANES), lambda bh, qi: (0, col(bh, 2))),
                  _resident((1, V_HEAD_DIM))],
        out_specs=pl.BlockSpec((tq, LANES), lambda bh, qi: (qi, bh)),
        out_shape=jax.ShapeDtypeStruct((seq, BATCH * D_MODEL), BF16),
        compiler_params=_params(("arbitrary", "arbitrary")),
        name="diff_attention",
    )(lam_vecs, view, view, view, subln_g)
    return out.reshape(seq * BATCH, D_MODEL)


def _softplus(z):
    return jnp.maximum(z, 0.0) + jnp.log1p(jnp.exp(-jnp.abs(z)))


def _rglru_kernel(cur_f, prev_f, next_f, cur_b, prev_b, next_b, cw_ref, cb_ref, wg_ref, bg_ref,
                  lam_ref, hf_ref, hb_ref, a_scr, carry_scr, *, steps):
    i = pl.program_id(0)
    n = pl.num_programs(0)
    rows = steps * BATCH

    @pl.when(i == 0)
    def _():
        carry_scr[...] = jnp.zeros_like(carry_scr)

    cw = cw_ref[...]
    cb = cb_ref[...]

    def conv(cur, prev, nxt, chunk):
        x = cur[...]
        p = prev[...] * (chunk > 0).astype(F32)
        q = nxt[...] * (chunk < n - 1).astype(F32)
        xm2 = jnp.concatenate([p, x[:rows - 2 * BATCH]], axis=0)
        xm1 = jnp.concatenate([p[BATCH:], x[:rows - BATCH]], axis=0)
        xp1 = jnp.concatenate([x[BATCH:], q], axis=0)
        return cb + xm2 * cw[0:1] + xm1 * cw[1:2] + x * cw[2:3] + xp1 * cw[3:4]

    def direction(d, xc, out_ref, reverse):
        neg_c_sp = -RG_C * _softplus(-lam_ref[d:d + 1, :])
        for g in range(N_GATE_GROUPS):
            c0 = g * GATE_GROUP
            xg = xc[:, c0:c0 + GATE_GROUP]
            pre = jnp.dot(xg.astype(BF16), wg_ref[d, g], preferred_element_type=F32)
            r = jax.nn.sigmoid(pre[:, :GATE_GROUP] + bg_ref[2 * d:2 * d + 1, c0:c0 + GATE_GROUP])
            gi = jax.nn.sigmoid(pre[:, GATE_GROUP:] + bg_ref[2 * d + 1:2 * d + 2, c0:c0 + GATE_GROUP])
            log_a = neg_c_sp[:, c0:c0 + GATE_GROUP] * r
            a = jnp.exp(log_a)
            a_scr[:, c0:c0 + GATE_GROUP] = a
            out_ref[:, c0:c0 + GATE_GROUP] = jnp.sqrt(1.0 - a * a) * gi * xg

        def body(s, h):
            t = (steps - 1 - s) if reverse else s
            r0 = pl.multiple_of(t * BATCH, BATCH)
            h = a_scr[pl.ds(r0, BATCH), :] * h + out_ref[pl.ds(r0, BATCH), :]
            out_ref[pl.ds(r0, BATCH), :] = h
            return h

        carry_scr[d] = lax.fori_loop(0, steps, body, carry_scr[d], unroll=8)

    direction(0, conv(cur_f, prev_f, next_f, i), hf_ref, False)
    direction(1, conv(cur_b, prev_b, next_b, n - 1 - i), hb_ref, True)


def _rglru(rest, conv_w, conv_b, wg, bg, lam, seq):
    total = seq * BATCH
    steps = min(SCAN_STEPS, seq)
    rows = steps * BATCH
    nchunks = seq // steps
    last8 = total // BATCH - 1

    def cur(idx):
        return pl.BlockSpec((rows, D_MODEL), lambda i: (idx(i), 0))

    def prev(idx):
        return pl.BlockSpec((2 * BATCH, D_MODEL),
                            lambda i: (jnp.maximum(idx(i) * (steps // 2) - 1, 0), 0))

    def nxt(idx):
        return pl.BlockSpec((BATCH, D_MODEL),
                            lambda i: (jnp.minimum((idx(i) + 1) * steps, last8), 0))

    fwd = lambda i: i
    bwd = lambda i: nchunks - 1 - i
    out_sds = jax.ShapeDtypeStruct((total, D_MODEL), F32)
    return pl.pallas_call(
        functools.partial(_rglru_kernel, steps=steps),
        grid=(nchunks,),
        in_specs=[cur(fwd), prev(fwd), nxt(fwd), cur(bwd), prev(bwd), nxt(bwd),
                  _resident((4, D_MODEL)), _resident((1, D_MODEL)),
                  _resident((2, N_GATE_GROUPS, GATE_GROUP, 2 * GATE_GROUP)),
                  _resident((4, D_MODEL)), _resident((2, D_MODEL))],
        out_specs=[cur(fwd), cur(bwd)],
        out_shape=[out_sds, out_sds],
        scratch_shapes=[pltpu.VMEM((rows, D_MODEL), F32), pltpu.VMEM((2, BATCH, D_MODEL), F32)],
        compiler_params=_params(("arbitrary",)),
        name="rglru",
    )(rest, rest, rest, rest, rest, rest, conv_w, conv_b, wg, bg, lam)


def _merge_kernel(att_ref, hf_ref, hb_ref, yr_ref, g_ref, x_ref, bgate_ref, wa_ref, wr_ref,
                  wo_ref, o_ref):
    rec = ((hf_ref[...] + hb_ref[...]) * jax.nn.gelu(yr_ref[...])).astype(BF16)
    a = jnp.dot(att_ref[...], wa_ref[...], preferred_element_type=F32)
    r = jnp.dot(rec, wr_ref[...], preferred_element_type=F32)
    gates = jax.nn.sigmoid(g_ref[...] + bgate_ref[...])
    merged = (gates[:, :D_MODEL] * a + gates[:, D_MODEL:] * r).astype(BF16)
    o_ref[...] = x_ref[...] + jnp.dot(merged, wo_ref[...], preferred_element_type=F32)


def _merge(att, hf, hb, rest, x, b_gate, wa, wr, wo):
    rows = x.shape[0]
    tm = min(ROW_TILE, rows)
    row_spec = lambda width, col=0: pl.BlockSpec((tm, width), lambda i: (i, col))
    sq = _resident((D_MODEL, D_MODEL))
    return pl.pallas_call(
        _merge_kernel,
        grid=(rows // tm,),
        in_specs=[row_spec(D_MODEL), row_spec(D_MODEL), row_spec(D_MODEL),
                  row_spec(D_MODEL, 1), row_spec(2 * D_MODEL, 1), row_spec(D_MODEL),
                  _resident((1, 2 * D_MODEL)), sq, sq, sq],
        out_specs=row_spec(D_MODEL),
        out_shape=jax.ShapeDtypeStruct((rows, D_MODEL), F32),
        compiler_params=_params(("arbitrary",)),
        name="merge",
    )(att, hf, hb, rest, rest, x, b_gate, wa, wr, wo)


def _ffn_kernel(x_ref, g2_ref, w1_ref, w2_ref, o_ref):
    x = x_ref[...]
    ms = jnp.mean(x * x, axis=-1, keepdims=True)
    h = (x * lax.rsqrt(ms + EPS) * g2_ref[...]).astype(BF16)
    acc = x
    for c in range(D_FF // D_MODEL):
        f = jnp.dot(h, w1_ref[:, c * D_MODEL:(c + 1) * D_MODEL], preferred_element_type=F32)
        f = jnp.square(jnp.maximum(f, 0.0)).astype(BF16)
        acc = acc + jnp.dot(f, w2_ref[c * D_MODEL:(c + 1) * D_MODEL, :], preferred_element_type=F32)
    o_ref[...] = acc


def _ffn(x, g2, w1, w2):
    rows = x.shape[0]
    tm = min(ROW_TILE, rows)
    row_spec = pl.BlockSpec((tm, D_MODEL), lambda i: (i, 0))
    return pl.pallas_call(
        _ffn_kernel,
        grid=(rows // tm,),
        in_specs=[row_spec, _resident((1, D_MODEL)), _resident((D_MODEL, D_FF)),
                  _resident((D_FF, D_MODEL))],
        out_specs=row_spec,
        out_shape=jax.ShapeDtypeStruct((rows, D_MODEL), F32),
        compiler_params=_params(("arbitrary",)),
        name="ffn",
    )(x, g2, w1, w2)


def _rope_tables(seq):
    pos = jnp.arange(seq, dtype=F32)
    inv_freq = ROPE_THETA ** (-jnp.arange(0, ROT_DIM, 2, dtype=F32) / ROT_DIM)
    ang = pos[:, None] * inv_freq[None, :]
    cos, sin = jnp.cos(ang), jnp.sin(ang)
    rest = HEAD_DIM - ROT_DIM
    cos64 = jnp.concatenate([cos, cos, jnp.ones((seq, rest), F32)], axis=-1)
    sin64 = jnp.concatenate([-sin, sin, jnp.zeros((seq, rest), F32)], axis=-1)
    expand = lambda t: jnp.repeat(jnp.tile(t, (1, LANES // HEAD_DIM)), BATCH, axis=0)
    return expand(cos64), expand(sin64)


def _gate_weights(rg_w):
    per = GATE_GROUP // RNN_BLOCK
    blocks = rg_w.reshape(2, 2, N_GATE_GROUPS, per, RNN_BLOCK, RNN_BLOCK)
    eye = jnp.eye(per, dtype=rg_w.dtype)
    bd = jnp.einsum('dtgjcn,jk->dtgjckn', blocks, eye)
    bd = bd.reshape(2, 2, N_GATE_GROUPS, GATE_GROUP, GATE_GROUP)
    return bd.transpose(0, 2, 3, 1, 4).reshape(2, N_GATE_GROUPS, GATE_GROUP, 2 * GATE_GROUP).astype(BF16)


def _prepare(norm1_g, w_in, b_gate, q_norm_g, k_norm_g, lam_vecs, subln_g, conv_w, conv_b, rg_w,
             rg_b, rg_L, w_branch_att, w_branch_rec, w_out, norm2_g, w_ff1, w_ff2):
    per = GATE_GROUP // HEAD_DIM
    ones_bd = jnp.kron(jnp.eye(per, dtype=F32), jnp.ones((HEAD_DIM, HEAD_DIM), F32)).astype(BF16)
    layers = []
    for l in range(DEPTH):
        scale = HEAD_DIM ** -0.5
        qk_gain = jnp.stack([jnp.tile(q_norm_g[l] * scale, LANES // HEAD_DIM),
                             jnp.tile(k_norm_g[l], LANES // HEAD_DIM)]).reshape(2, 1, LANES)
        layers.append(dict(
            lam_init=0.8 - 0.6 * math.exp(-0.3 * l),
            g1=norm1_g[l].reshape(1, D_MODEL), w_in=w_in[l].astype(BF16), qk_gain=qk_gain,
            ones_bd=ones_bd, lam_vecs=lam_vecs[l], subln_g=subln_g[l].reshape(1, V_HEAD_DIM),
            conv_w=conv_w[l], conv_b=conv_b[l].reshape(1, D_MODEL), wg=_gate_weights(rg_w[l]),
            bg=rg_b[l].reshape(4, D_MODEL), lam=rg_L[l], b_gate=b_gate[l].reshape(1, 2 * D_MODEL),
            wa=w_branch_att[l].astype(BF16), wr=w_branch_rec[l].astype(BF16),
            wo=w_out[l].astype(BF16), g2=norm2_g[l].reshape(1, D_MODEL),
            w1=w_ff1[l].astype(BF16), w2=w_ff2[l].astype(BF16)))
    return layers


def _trunk(x, layers):
    batch, seq, d = x.shape
    assert batch == BATCH and d == D_MODEL
    cos_t, sin_t = _rope_tables(seq)
    xt = x.transpose(1, 0, 2).reshape(seq * batch, d)
    for p in layers:
        qkv, rest = _in_proj(xt, p["g1"], p["w_in"], p["qk_gain"], p["ones_bd"], cos_t, sin_t)
        att = _attention(qkv, p["lam_vecs"], p["subln_g"], p["lam_init"], seq)
        hf, hb = _rglru(rest, p["conv_w"], p["conv_b"], p["wg"], p["bg"], p["lam"], seq)
        xt = _merge(att, hf, hb, rest, xt, p["b_gate"], p["wa"], p["wr"], p["wo"])
        xt = _ffn(xt, p["g2"], p["w1"], p["w2"])
    return xt.reshape(seq, batch, d).transpose(1, 0, 2)


def kernel(x_prompt, x_sample, norm1_g, w_in, b_gate, q_norm_g, k_norm_g, lam_vecs, subln_g, conv_w, conv_b, rg_w, rg_b, rg_L, w_branch_att, w_branch_rec, w_out, norm2_g, w_ff1, w_ff2):
    layers = _prepare(norm1_g, w_in, b_gate, q_norm_g, k_norm_g, lam_vecs, subln_g, conv_w, conv_b,
                      rg_w, rg_b, rg_L, w_branch_att, w_branch_rec, w_out, norm2_g, w_ff1, w_ff2)
    return (_trunk(x_prompt, layers), _trunk(x_sample, layers))
```
